```python
import jax, jax.numpy as jnp
from jax import lax
import numpy as np

D_MODEL = 1024
BATCH = 2
SEQ = 16384
DEPTH = 2

D_MIX = 2 * D_MODEL
D_CONV = D_MIX // 2
CONV_GROUPS = 16
SHORT_CONV = 3
D_SSM = D_MIX - D_CONV
SSM_HEADDIM = 64
SSM_HEADS = D_SSM // SSM_HEADDIM
SSM_GROUPS = 2
SSM_STATE = 128
SSM_CONV = 4
SSM_CHUNK = 128
D_XBC = D_SSM + 2 * SSM_GROUPS * SSM_STATE
D_IN = 3 * D_CONV + D_SSM + D_XBC + SSM_HEADS
SPLITS = [D_CONV, 2 * D_CONV, 3 * D_CONV, 3 * D_CONV + D_SSM, 3 * D_CONV + D_SSM + D_XBC]
PEER_HEADS = 8
PEER_KEYS = 128
PEER_EXPERTS = PEER_KEYS * PEER_KEYS
PEER_TOPK = 16
PEER_DHALF = 128
PEER_BLOCK = 128
PLE_DIM = 256
EPS = 1e-6

kernel_name = "hymba_conv_ssd_peer_trunk"


def rmsnorm(x, w):
    xf = x.astype(jnp.float32)
    y = xf * lax.rsqrt(jnp.mean(xf * xf, axis=-1, keepdims=True) + EPS)
    return (y * w.astype(jnp.float32)).astype(x.dtype)


def causal_dwconv(x, w):
    k = w.shape[0]
    return lax.conv_general_dilated(
        x, w[:, None, :].astype(x.dtype), window_strides=(1,), padding=[(k - 1, 0)],
        dimension_numbers=("NWC", "WIO", "NWC"), feature_group_count=x.shape[-1])


def ssd_scan(x, dt, A, Bm, Cm, D):
    b, s, h, pdim = x.shape
    g, n = Bm.shape[2], Bm.shape[3]
    hg = h // g
    c, l = s // SSM_CHUNK, SSM_CHUNK
    f32 = jnp.float32
    xf = x.astype(f32).reshape(b, c, l, g, hg, pdim)
    dtf = dt.astype(f32).reshape(b, c, l, g, hg)
    Bf = Bm.astype(f32).reshape(b, c, l, g, n)
    Cf = Cm.astype(f32).reshape(b, c, l, g, n)
    xdt = xf * dtf[..., None]
    a = (dtf * A.astype(f32).reshape(g, hg)).transpose(0, 3, 4, 1, 2)
    a_cum = jnp.cumsum(a, axis=-1)
    causal = jnp.tril(jnp.ones((l, l), dtype=bool))
    L = jnp.exp(jnp.where(causal, a_cum[..., :, None] - a_cum[..., None, :], -jnp.inf))
    cb = jnp.einsum('bclgn,bcsgn->bgcls', Cf, Bf)
    y_diag = jnp.einsum('bghcls,bcsghp->bclghp', cb[:, :, None] * L, xdt)
    decay_in = jnp.exp(a_cum[..., -1:] - a_cum).transpose(0, 3, 4, 1, 2)
    states = jnp.einsum('bclgn,bclghp->bcghpn', Bf, xdt * decay_in[..., None])
    chunk_decay = jnp.exp(a_cum[..., -1]).transpose(3, 0, 1, 2)

    def step(carry, inp):
        st, dec = inp
        return carry * dec[..., None, None] + st, carry

    _, s_in = lax.scan(step, jnp.zeros_like(states[:, 0]),
                       (states.transpose(1, 0, 2, 3, 4, 5), chunk_decay))
    s_in = s_in.transpose(1, 0, 2, 3, 4, 5)
    decay_out = jnp.exp(a_cum).transpose(0, 3, 4, 1, 2)
    y_off = jnp.einsum('bclgn,bcghpn->bclghp', Cf, s_in) * decay_out[..., None]
    y = y_diag + y_off + xf * D.astype(f32).reshape(g, hg)[..., None]
    return y.reshape(b, s, h * pdim).astype(x.dtype)


def hybrid_mixer(xn, w_in, conv_w, mconv_w, mconv_b, dt_bias, a_log, d_skip,
                 conv_norm_w, ssd_norm_w, w_out):
    b, s, _ = xn.shape
    proj = xn @ w_in
    gb, gc, hc, z, xbc, dt = jnp.split(proj, SPLITS, axis=-1)
    y_conv = rmsnorm(gb * causal_dwconv(gc * hc, conv_w), conv_norm_w)
    xbc = jax.nn.silu(causal_dwconv(xbc, mconv_w) + mconv_b)
    xs, Bm, Cm = jnp.split(xbc, [D_SSM, D_SSM + SSM_GROUPS * SSM_STATE], axis=-1)
    dt = jax.nn.softplus(dt.astype(jnp.float32) + dt_bias.astype(jnp.float32))
    A = -jnp.exp(a_log.astype(jnp.float32))
    y_ssd = ssd_scan(xs.reshape(b, s, SSM_HEADS, SSM_HEADDIM), dt, A,
                     Bm.reshape(b, s, SSM_GROUPS, SSM_STATE),
                     Cm.reshape(b, s, SSM_GROUPS, SSM_STATE), d_skip)
    y_ssd = rmsnorm(y_ssd * jax.nn.silu(z), ssd_norm_w)
    return jnp.concatenate([y_conv, y_ssd], axis=-1) @ w_out


def peer_ffn(xn, wq, keys, u, v):
    b, s, d = xn.shape
    xt = xn.reshape(-1, PEER_BLOCK, d)
    keys_f = keys.astype(jnp.float32)

    def block(xb):
        t = xb.shape[0]
        q = (xb @ wq).reshape(t, PEER_HEADS, 2, PEER_DHALF).astype(jnp.float32)
        sc = jnp.einsum('thkd,hknd->thkn', q, keys_f)
        sv, si = lax.top_k(sc, PEER_TOPK)
        cand = (sv[:, :, 0, :, None] + sv[:, :, 1, None, :]).reshape(t, PEER_HEADS, PEER_TOPK * PEER_TOPK)
        tv, ti = lax.top_k(cand, PEER_TOPK)
        i1 = jnp.take_along_axis(si[:, :, 0], ti // PEER_TOPK, axis=-1)
        i2 = jnp.take_along_axis(si[:, :, 1], ti % PEER_TOPK, axis=-1)
        eidx = (i1 * PEER_KEYS + i2).reshape(t, PEER_HEADS * PEER_TOPK)
        gate = jax.nn.softmax(tv, axis=-1).reshape(t, PEER_HEADS * PEER_TOPK)
        ue = jnp.take(u, eidx, axis=0)
        ve = jnp.take(v, eidx, axis=0)
        act = jax.nn.gelu(jnp.einsum('tkd,td->tk', ue, xb))
        return jnp.einsum('tk,tkd->td', (gate * act).astype(xb.dtype), ve)

    return lax.map(block, xt).reshape(b, s, d)


def setup_inputs(seed: int = 0) -> dict:
    key = jax.random.key(seed)
    ks = jax.random.split(key, 24)
    f32 = jnp.float32
    nrm = lambda k, shape, scale: (jax.random.normal(k, shape, f32) * scale).astype(f32)
    gain = lambda k, shape: 1.0 + 0.05 * jax.random.normal(k, shape, f32)
    dt0 = jnp.exp(jax.random.uniform(ks[7], (DEPTH, SSM_HEADS), f32) * (jnp.log(0.1) - jnp.log(0.001)) + jnp.log(0.001))
    return {
        "x": nrm(ks[0], (BATCH, SEQ, D_MODEL), 1.0),
        "p": nrm(ks[1], (DEPTH, BATCH, SEQ, PLE_DIM), 1.0),
        "norm_mix_w": gain(ks[2], (DEPTH, D_MODEL)),
        "w_in": nrm(ks[3], (DEPTH, D_MODEL, D_IN), D_MODEL ** -0.5),
        "conv_w": nrm(ks[4], (DEPTH, SHORT_CONV, D_CONV), SHORT_CONV ** -0.5),
        "mconv_w": nrm(ks[5], (DEPTH, SSM_CONV, D_XBC), SSM_CONV ** -0.5),
        "mconv_b": nrm(ks[6], (DEPTH, D_XBC), 0.01),
        "dt_bias": dt0 + jnp.log(-jnp.expm1(-dt0)),
        "a_log": jnp.log(jax.random.uniform(ks[8], (DEPTH, SSM_HEADS), f32, 1.0, 16.0)),
        "d_skip": gain(ks[9], (DEPTH, SSM_HEADS)),
        "conv_norm_w": gain(ks[10], (DEPTH, D_CONV)),
        "ssd_norm_w": gain(ks[11], (DEPTH, D_SSM)),
        "w_out": nrm(ks[12], (DEPTH, D_MIX, D_MODEL), D_MIX ** -0.5),
        "norm_ffn_w": gain(ks[13], (DEPTH, D_MODEL)),
        "peer_wq": nrm(ks[14], (DEPTH, D_MODEL, PEER_HEADS * 2 * PEER_DHALF), D_MODEL ** -0.5),
        "peer_keys": nrm(ks[15], (DEPTH, PEER_HEADS, 2, PEER_KEYS, PEER_DHALF), PEER_DHALF ** -0.5),
        "peer_u": nrm(ks[16], (DEPTH, PEER_EXPERTS, D_MODEL), D_MODEL ** -0.5),
        "peer_v": nrm(ks[17], (DEPTH, PEER_EXPERTS, D_MODEL), PEER_HEADS ** -0.5),
        "norm_ple_w": gain(ks[18], (DEPTH, D_MODEL)),
        "w_ple_gate": nrm(ks[19], (DEPTH, D_MODEL, D_MODEL), D_MODEL ** -0.5),
        "w_ple": nrm(ks[20], (DEPTH, PLE_DIM, D_MODEL), PLE_DIM ** -0.5),
        "final_norm_w": gain(ks[21], (D_MODEL,)),
    }


def reference(x, p, norm_mix_w, w_in, conv_w, mconv_w, mconv_b, dt_bias, a_log, d_skip,
              conv_norm_w, ssd_norm_w, w_out, norm_ffn_w, peer_wq, peer_keys, peer_u, peer_v,
              norm_ple_w, w_ple_gate, w_ple, final_norm_w):
    h = x
    for i in range(DEPTH):
        h = h + hybrid_mixer(rmsnorm(h, norm_mix_w[i]), w_in[i], conv_w[i], mconv_w[i], mconv_b[i],
                             dt_bias[i], a_log[i], d_skip[i], conv_norm_w[i], ssd_norm_w[i], w_out[i])
        h = h + peer_ffn(rmsnorm(h, norm_ffn_w[i]), peer_wq[i], peer_keys[i], peer_u[i], peer_v[i])
        gate = jax.nn.sigmoid(rmsnorm(h, norm_ple_w[i]) @ w_ple_gate[i])
        h = h + gate * (p[i] @ w_ple[i])
    return rmsnorm(h, final_norm_w)
```

```python
import functools

import jax
import jax.numpy as jnp
from jax import lax
from jax.experimental import pallas as pl
from jax.experimental.pallas import tpu as pltpu

F32 = jnp.float32
BF16 = jnp.bfloat16

LANES = 128
SUBLANES = 8
VMEM_LIMIT_BYTES = 56 * 1024 * 1024

D_MODEL = 1024
D_CONV = 1024
SHORT_CONV = 3
D_SSM = 1024
SSM_HEADDIM = 64
SSM_HEADS = D_SSM // SSM_HEADDIM
SSM_GROUPS = 2
SSM_STATE = 128
SSM_CONV = 4
SSM_CHUNK = 128
D_XBC = D_SSM + 2 * SSM_GROUPS * SSM_STATE
D_MAIN = 3 * D_CONV + D_SSM + D_XBC
D_IN_PAD = D_MAIN + LANES
PEER_HEADS = 8
PEER_KEYS = 128
PEER_TOPK = 16
TOPK_SHIFT = PEER_TOPK.bit_length() - 1
PEER_DHALF = 128
PEER_PICKS = PEER_HEADS * PEER_TOPK
PLE_DIM = 256
EPS = 1e-6
CONV_TAIL = SUBLANES


def _rms(x, w):
    return x * lax.rsqrt(jnp.mean(x * x, axis=-1, keepdims=True) + EPS) * w


def _split3(x):
    x1 = x.astype(BF16)
    r1 = x - x1.astype(F32)
    x2 = r1.astype(BF16)
    x3 = (r1 - x2.astype(F32)).astype(BF16)
    return x1, x2, x3


def _dot(a, b):
    return jnp.dot(a, b, preferred_element_type=F32)


def _dot_nt(a, b):
    return lax.dot_general(a, b, (((1,), (1,)), ((), ())), preferred_element_type=F32)


def _sel_right(x, m):
    x1, x2, x3 = _split3(x)
    return _dot(x1, m) + _dot(x2, m) + _dot(x3, m)


def _sel_left(m, x):
    x1, x2, x3 = _split3(x)
    return _dot(m, x1) + _dot(m, x2) + _dot(m, x3)


def _sel_right_nt(x, m):
    x1, x2, x3 = _split3(x)
    return _dot_nt(x1, m) + _dot_nt(x2, m) + _dot_nt(x3, m)


def _mixer_body(h_ref, nw_ref, win_ref, cw_ref, mw_ref, mb_ref, dtb_ref, alog_ref, dexp_ref, cnw_ref, snw_ref,
                wout_ref, tril_ref, e64_ref, e128_ref, o_ref,
                proj_scr, gch_scr, xraw_scr, xbc_scr, y_scr, st_scr, *, ts):
    j = pl.program_id(1)

    @pl.when(j == 0)
    def _():
        gch_scr[0:CONV_TAIL, :] = jnp.zeros((CONV_TAIL, D_CONV), F32)
        xraw_scr[0:CONV_TAIL, :] = jnp.zeros((CONV_TAIL, D_XBC), F32)
        st_scr[...] = jnp.zeros(st_scr.shape, F32)

    x = h_ref[...]
    xn = _rms(x, nw_ref[...])
    proj_scr[...] = _dot(xn.astype(BF16), win_ref[...])

    gch_scr[CONV_TAIL:CONV_TAIL + ts, :] = proj_scr[:, D_CONV:2 * D_CONV] * proj_scr[:, 2 * D_CONV:3 * D_CONV]
    conv = cw_ref[0:1, :] * gch_scr[CONV_TAIL - 2:CONV_TAIL - 2 + ts, :]
    for jj in range(1, SHORT_CONV):
        off = CONV_TAIL - (SHORT_CONV - 1) + jj
        conv = conv + cw_ref[jj:jj + 1, :] * gch_scr[off:off + ts, :]
    y_conv = _rms(proj_scr[:, 0:D_CONV] * conv, cnw_ref[...])
    gch_scr[0:CONV_TAIL, :] = gch_scr[ts:ts + CONV_TAIL, :]

    xraw_scr[CONV_TAIL:CONV_TAIL + ts, :] = proj_scr[:, 4 * D_CONV:4 * D_CONV + D_XBC]
    xc = mb_ref[...] + mw_ref[0:1, :] * xraw_scr[CONV_TAIL - 3:CONV_TAIL - 3 + ts, :]
    for jj in range(1, SSM_CONV):
        off = CONV_TAIL - (SSM_CONV - 1) + jj
        xc = xc + mw_ref[jj:jj + 1, :] * xraw_scr[off:off + ts, :]
    xbc_scr[...] = xc * jax.nn.sigmoid(xc)
    xraw_scr[0:CONV_TAIL, :] = xraw_scr[ts:ts + CONV_TAIL, :]

    tril = tril_ref[...]
    causal = tril > 0
    a_neg = -jnp.exp(alog_ref[...])
    half = SSM_HEADS // SSM_GROUPS * SSM_HEADDIM
    lane = lax.broadcasted_iota(jnp.int32, (SSM_CHUNK, LANES), 1)

    for c in range(ts // SSM_CHUNK):
        r0 = c * SSM_CHUNK
        rows = slice(r0, r0 + SSM_CHUNK)
        xs = xbc_scr[rows, 0:D_SSM]
        dt_in = proj_scr[rows, D_MAIN:D_MAIN + LANES] + dtb_ref[...]
        dt = jnp.maximum(dt_in, 0.0) + jnp.log1p(jnp.exp(-jnp.abs(dt_in)))
        a = dt * a_neg
        a_cum = _sel_left(tril, a)
        a_cum_t = _sel_right_nt(a.T, tril)
        tot = a_cum[SSM_CHUNK - 1:SSM_CHUNK, :]
        dt_e = _sel_right(dt, e64_ref[...])
        dec_in_e = _sel_right(jnp.exp(tot - a_cum), e64_ref[...])
        dec_out_e = _sel_right(jnp.exp(a_cum), e64_ref[...])
        cd_e = _sel_right(jnp.broadcast_to(jnp.exp(tot), (SUBLANES, LANES)), e64_ref[...])[0:1, :]
        col_b = _sel_right(a_cum, e128_ref[...])
        xdt = xs * dt_e
        xdt_b = xdt.astype(BF16)
        xdtd_b = (xdt * dec_in_e).astype(BF16)
        for g in range(SSM_GROUPS):
            bg = xbc_scr[rows, D_SSM + g * SSM_STATE:D_SSM + (g + 1) * SSM_STATE]
            cg = xbc_scr[rows, D_SSM + (SSM_GROUPS + g) * SSM_STATE:D_SSM + (SSM_GROUPS + g + 1) * SSM_STATE]
            bg_b = bg.astype(BF16)
            cg_b = cg.astype(BF16)
            cb = _dot_nt(cg_b, bg_b)
            gcols = slice(g * half, (g + 1) * half)
            s_in = st_scr[g]
            y_off = _dot(cg_b, s_in.astype(BF16)) * dec_out_e[:, gcols]
            st_scr[g] = s_in * cd_e[:, gcols] + _dot(bg.T.astype(BF16), xdtd_b[:, gcols])
            for hp in range(SSM_HEADS // SSM_GROUPS // 2):
                h0 = g * (SSM_HEADS // SSM_GROUPS) + 2 * hp
                pcols = slice(h0 * SSM_HEADDIM, h0 * SSM_HEADDIM + LANES)
                res = []
                for hh in (h0, h0 + 1):
                    seg = col_b[:, hh * LANES:(hh + 1) * LANES] - a_cum_t[hh:hh + 1, :]
                    decay = jnp.exp(jnp.where(causal, seg, -jnp.inf))
                    res.append(_dot((cb * decay).astype(BF16), xdt_b[:, pcols]))
                y_diag = jnp.where(lane < SSM_HEADDIM, res[0], res[1])
                ocols = slice(hp * LANES, (hp + 1) * LANES)
                y_scr[rows, pcols] = y_diag + y_off[:, ocols] + xs[:, pcols] * dexp_ref[:, pcols]

    z = proj_scr[:, 3 * D_CONV:3 * D_CONV + D_SSM]
    y_ssd = _rms(y_scr[...] * (z * jax.nn.sigmoid(z)), snw_ref[...])
    out = _dot(y_conv.astype(BF16), wout_ref[0:D_CONV, :]) + _dot(y_ssd.astype(BF16), wout_ref[D_CONV:D_CONV + D_SSM, :])
    o_ref[...] = x + out


def _mixer(h, batch, seq, nw, win_b, cw, mw, mb, dtb, alog, dexp, cnw, snw, wout_b, tril, e64, e128, *, ts):
    n = batch * seq
    nts = seq // ts
    const = lambda shape: pl.BlockSpec(shape, lambda b, j: (0,) * len(shape))
    row = pl.BlockSpec((ts, D_MODEL), lambda b, j: (b * nts + j, 0))
    return pl.pallas_call(
        functools.partial(_mixer_body, ts=ts),
        grid=(batch, nts),
        in_specs=[row, const((1, D_MODEL)), const((D_MODEL, D_IN_PAD)), const((SUBLANES, D_CONV)),
                  const((SUBLANES, D_XBC)), const((1, D_XBC)), const((1, LANES)), const((1, LANES)),
                  const((1, D_SSM)), const((1, D_CONV)), const((1, D_SSM)), const((D_CONV + D_SSM, D_MODEL)),
                  const((SSM_CHUNK, SSM_CHUNK)), const((LANES, D_SSM)), const((LANES, SSM_HEADS * LANES))],
        out_specs=row,
        out_shape=jax.ShapeDtypeStruct((n, D_MODEL), F32),
        scratch_shapes=[pltpu.VMEM((ts, D_IN_PAD), F32), pltpu.VMEM((ts + CONV_TAIL, D_CONV), F32),
                        pltpu.VMEM((ts + CONV_TAIL, D_XBC), F32), pltpu.VMEM((ts, D_XBC), F32),
                        pltpu.VMEM((ts, D_SSM), F32),
                        pltpu.VMEM((SSM_GROUPS, SSM_STATE, D_SSM // SSM_GROUPS), F32)],
        compiler_params=pltpu.CompilerParams(dimension_semantics=("arbitrary", "arbitrary"),
                                             vmem_limit_bytes=VMEM_LIMIT_BYTES),
        name="mixer",
    )(h, nw, win_b, cw, mw, mb, dtb, alog, dexp, cnw, snw, wout_b, tril, e64, e128)


def _topk_rows(s, k):
    nrows = s.shape[0]
    iota = lax.broadcasted_iota(jnp.int32, s.shape, 0)
    vals, idxs = [], []
    for _ in range(k):
        m = jnp.max(s, axis=0, keepdims=True)
        idx = jnp.min(jnp.where(s == m, iota, nrows), axis=0, keepdims=True)
        vals.append(m)
        idxs.append(idx)
        s = jnp.where(iota == idx, -jnp.inf, s)
    return jnp.concatenate(vals, axis=0), jnp.concatenate(idxs, axis=0)


def _take_rows(table, sel):
    out = jnp.zeros(sel.shape, table.dtype)
    for jrow in range(table.shape[0]):
        out = jnp.where(sel == jrow, table[jrow:jrow + 1, :], out)
    return out


def _route_body(h_ref, nw_ref, wq_ref, keys_ref, xn_ref, eidx_ref, gate_ref, q_scr):
    xn = _rms(h_ref[...], nw_ref[...])
    xn_ref[...] = xn
    q_scr[...] = _dot(xn.astype(BF16), wq_ref[...]).astype(BF16)
    for hd in range(PEER_HEADS):
        sv, si = [], []
        for half in range(2):
            hc = hd * 2 + half
            sc_t = _dot_nt(keys_ref[hc], q_scr[:, hc * PEER_DHALF:(hc + 1) * PEER_DHALF])
            v, i = _topk_rows(sc_t, PEER_TOPK)
            sv.append(v)
            si.append(i)
        cand = jnp.concatenate([sv[0][a:a + 1, :] + sv[1] for a in range(PEER_TOPK)], axis=0)
        tv, ti = _topk_rows(cand, PEER_TOPK)
        i1 = _take_rows(si[0], lax.shift_right_logical(ti, TOPK_SHIFT))
        i2 = _take_rows(si[1], lax.bitwise_and(ti, PEER_TOPK - 1))
        ex = jnp.exp(tv - tv[0:1, :])
        rows = slice(hd * PEER_TOPK, (hd + 1) * PEER_TOPK)
        eidx_ref[0, rows, :] = i1 * PEER_KEYS + i2
        gate_ref[0, rows, :] = ex / jnp.sum(ex, axis=0, keepdims=True)


def _route(h, nw, wq_b, keys_b, *, tr):
    n = h.shape[0]
    nblk = n // tr
    return pl.pallas_call(
        _route_body,
        grid=(nblk,),
        in_specs=[pl.BlockSpec((tr, D_MODEL), lambda i: (i, 0)),
                  pl.BlockSpec((1, D_MODEL), lambda i: (0, 0)),
                  pl.BlockSpec((D_MODEL, 2 * PEER_HEADS * PEER_DHALF), lambda i: (0, 0)),
                  pl.BlockSpec((2 * PEER_HEADS, PEER_KEYS, PEER_DHALF), lambda i: (0, 0, 0))],
        out_specs=[pl.BlockSpec((tr, D_MODEL), lambda i: (i, 0)),
                   pl.BlockSpec((1, PEER_PICKS, tr), lambda i: (i, 0, 0)),
                   pl.BlockSpec((1, PEER_PICKS, tr), lambda i: (i, 0, 0))],
        out_shape=[jax.ShapeDtypeStruct((n, D_MODEL), F32),
                   jax.ShapeDtypeStruct((nblk, PEER_PICKS, tr), jnp.int32),
                   jax.ShapeDtypeStruct((nblk, PEER_PICKS, tr), F32)],
        scratch_shapes=[pltpu.VMEM((tr, 2 * PEER_HEADS * PEER_DHALF), BF16)],
        compiler_params=pltpu.CompilerParams(dimension_semantics=("arbitrary",),
                                             vmem_limit_bytes=VMEM_LIMIT_BYTES),
        name="route",
    )(h, nw, wq_b, keys_b)


ROW_TILES = D_MODEL // LANES


def _expert_body(idx_hbm, gate_hbm, xn_ref, h_ref, tab_hbm, o_ref,
                 idx_s, gate_s, buf, part_scr, w_scr, isem, gsem, rsem, *, g_tok, nsteps):
    i = pl.program_id(0)
    slot = lax.rem(i, 2)
    prev = 1 - slot
    npick = PEER_PICKS

    def idx_copy(grp, s):
        return pltpu.make_async_copy(idx_hbm.at[grp], idx_s.at[s], isem.at[s])

    def gate_copy(grp, s):
        return pltpu.make_async_copy(gate_hbm.at[grp], gate_s.at[s], gsem.at[s])

    def rows_done(s):
        return pltpu.make_async_copy(tab_hbm.at[pl.ds(0, g_tok * npick)], buf.at[s], rsem.at[s])

    @pl.when(i == 0)
    def _():
        idx_copy(0, 0).start()

    @pl.when(i < nsteps)
    def _():
        idx_copy(i, slot).wait()

        @pl.when(i + 1 < nsteps)
        def _():
            idx_copy(i + 1, prev).start()

        gate_copy(i, slot).start()

        def issue(t, carry):
            for k in range(npick):
                e = idx_s[slot, t, k]
                pltpu.make_async_copy(tab_hbm.at[e], buf.at[slot, t * npick + k], rsem.at[slot]).start()
            return carry

        lax.fori_loop(0, g_tok, issue, 0)

    @pl.when(i > 0)
    def _():
        gate_copy(i - 1, prev).wait()
        rows_done(prev).wait()

        def token(t, carry):
            xt = xn_ref[t]
            base = t * npick
            for k in range(npick):
                part_scr[k:k + 1, :] = jnp.sum(buf[prev, base + k, 0:ROW_TILES, :] * xt, axis=0, keepdims=True)
            act = jnp.sum(part_scr[...], axis=-1, keepdims=True)
            w_scr[...] = jnp.broadcast_to(jax.nn.gelu(act), (npick, LANES))
            accs = [jnp.zeros((ROW_TILES, LANES), F32) for _ in range(4)]
            for k in range(npick):
                coef = w_scr[k:k + 1, :] * gate_s[prev, t, k]
                accs[k % 4] = accs[k % 4] + coef * buf[prev, base + k, ROW_TILES:2 * ROW_TILES, :]
            o_ref[t] = h_ref[t] + ((accs[0] + accs[1]) + (accs[2] + accs[3]))
            return carry

        lax.fori_loop(0, g_tok, token, 0)


def _experts(eidx, gate, xn, h, table, *, g_tok):
    n = xn.shape[0]
    nsteps = n // g_tok
    idx3 = eidx.reshape(nsteps, g_tok, PEER_PICKS)
    gate3 = gate.reshape(nsteps, g_tok, PEER_PICKS)
    xn3 = xn.reshape(n, ROW_TILES, LANES)
    h3 = h.reshape(n, ROW_TILES, LANES)
    tok = pl.BlockSpec((g_tok, ROW_TILES, LANES), lambda i: (jnp.maximum(i - 1, 0), 0, 0))
    out = pl.pallas_call(
        functools.partial(_expert_body, g_tok=g_tok, nsteps=nsteps),
        grid=(nsteps + 1,),
        in_specs=[pl.BlockSpec(memory_space=pl.ANY), pl.BlockSpec(memory_space=pl.ANY), tok, tok,
                  pl.BlockSpec(memory_space=pl.ANY)],
        out_specs=tok,
        out_shape=jax.ShapeDtypeStruct((n, ROW_TILES, LANES), F32),
        scratch_shapes=[pltpu.SMEM((2, g_tok, PEER_PICKS), jnp.int32),
                        pltpu.SMEM((2, g_tok, PEER_PICKS), F32),
                        pltpu.VMEM((2, g_tok * PEER_PICKS, 2 * ROW_TILES, LANES), F32),
                        pltpu.VMEM((PEER_PICKS, LANES), F32),
                        pltpu.VMEM((PEER_PICKS, LANES), F32),
                        pltpu.SemaphoreType.DMA((2,)), pltpu.SemaphoreType.DMA((2,)),
                        pltpu.SemaphoreType.DMA((2,))],
        compiler_params=pltpu.CompilerParams(dimension_semantics=("arbitrary",),
                                             vmem_limit_bytes=VMEM_LIMIT_BYTES),
        name="experts",
    )(idx3, gate3, xn3, h3, table)
    return out.reshape(n, D_MODEL)


def _ple_body(h_ref, p_ref, nw_ref, wg_ref, wp_ref, fw_ref, o_ref, *, final):
    x = h_ref[...]
    gate = jax.nn.sigmoid(_dot(_rms(x, nw_ref[...]).astype(BF16), wg_ref[...]))
    out = x + gate * _dot(p_ref[...].astype(BF16), wp_ref[...])
    o_ref[...] = _rms(out, fw_ref[...]) if final else out


def _ple(h, p, nw, wg_b, wp_b, fw, *, final, tp):
    n = h.shape[0]
    row = lambda d: pl.BlockSpec((tp, d), lambda i: (i, 0))
    const = lambda shape: pl.BlockSpec(shape, lambda i: (0, 0))
    return pl.pallas_call(
        functools.partial(_ple_body, final=final),
        grid=(n // tp,),
        in_specs=[row(D_MODEL), row(PLE_DIM), const((1, D_MODEL)), const((D_MODEL, D_MODEL)),
                  const((PLE_DIM, D_MODEL)), const((1, D_MODEL))],
        out_specs=row(D_MODEL),
        out_shape=jax.ShapeDtypeStruct((n, D_MODEL), F32),
        compiler_params=pltpu.CompilerParams(dimension_semantics=("arbitrary",),
                                             vmem_limit_bytes=VMEM_LIMIT_BYTES),
        name="ple",
    )(h, p, nw, wg_b, wp_b, fw)


def _tile_sizes(seq):
    ts = 256 if seq % 256 == 0 else SSM_CHUNK
    return dict(ts=ts, tr=256 if seq % 256 == 0 else LANES, g_tok=SUBLANES, tp=512 if seq % 512 == 0 else LANES)


def _pad_rows(w, rows):
    return jnp.pad(w, ((0, rows - w.shape[0]), (0, 0)))


def _pad_lanes(v):
    return jnp.pad(v, (0, LANES - v.shape[0])).reshape(1, LANES)


def kernel(x, p, norm_mix_w, w_in, conv_w, mconv_w, mconv_b, dt_bias, a_log, d_skip, conv_norm_w, ssd_norm_w,
           w_out, norm_ffn_w, peer_wq, peer_keys, peer_u, peer_v, norm_ple_w, w_ple_gate, w_ple, final_norm_w):
    batch, seq, _ = x.shape
    depth = w_in.shape[0]
    n = batch * seq
    tiles = _tile_sizes(seq)
    n_exp = peer_u.shape[1]

    li = lax.broadcasted_iota(jnp.int32, (SSM_CHUNK, SSM_CHUNK), 0)
    si = lax.broadcasted_iota(jnp.int32, (SSM_CHUNK, SSM_CHUNK), 1)
    tril = (li >= si).astype(BF16)
    hrow = lax.broadcasted_iota(jnp.int32, (LANES, D_SSM), 0)
    hcol = lax.broadcasted_iota(jnp.int32, (LANES, D_SSM), 1)
    e64 = (hcol // SSM_HEADDIM == hrow).astype(BF16)
    hrow2 = lax.broadcasted_iota(jnp.int32, (LANES, SSM_HEADS * LANES), 0)
    hcol2 = lax.broadcasted_iota(jnp.int32, (LANES, SSM_HEADS * LANES), 1)
    e128 = (hcol2 // LANES == hrow2).astype(BF16)

    h = x.reshape(n, D_MODEL)
    for i in range(depth):
        win_b = jnp.pad(w_in[i], ((0, 0), (0, D_IN_PAD - w_in.shape[2]))).astype(BF16)
        h = _mixer(h, batch, seq, norm_mix_w[i].reshape(1, -1), win_b,
                   _pad_rows(conv_w[i], SUBLANES), _pad_rows(mconv_w[i], SUBLANES), mconv_b[i].reshape(1, -1),
                   _pad_lanes(dt_bias[i]), _pad_lanes(a_log[i]), jnp.repeat(d_skip[i], SSM_HEADDIM).reshape(1, -1),
                   conv_norm_w[i].reshape(1, -1), ssd_norm_w[i].reshape(1, -1), w_out[i].astype(BF16),
                   tril, e64, e128, ts=tiles["ts"])
        keys_b = peer_keys[i].reshape(2 * PEER_HEADS, PEER_KEYS, PEER_DHALF).astype(BF16)
        xn, eidx_t, gate_t = _route(h, norm_ffn_w[i].reshape(1, -1), peer_wq[i].astype(BF16), keys_b, tr=tiles["tr"])
        eidx = eidx_t.transpose(0, 2, 1).reshape(n, PEER_PICKS)
        gate = gate_t.transpose(0, 2, 1).reshape(n, PEER_PICKS)
        table = jnp.concatenate([peer_u[i].reshape(n_exp, ROW_TILES, LANES),
                                 peer_v[i].reshape(n_exp, ROW_TILES, LANES)], axis=1)
        h = _experts(eidx, gate, xn, h, table, g_tok=tiles["g_tok"])
        h = _ple(h, p[i].reshape(n, PLE_DIM), norm_ple_w[i].reshape(1, -1), w_ple_gate[i].astype(BF16),
                 w_ple[i].astype(BF16), final_norm_w.reshape(1, -1), final=(i == depth - 1), tp=tiles["tp"])
    return h.reshape(batch, seq, D_MODEL)
```

```python
import functools

import jax
import jax.numpy as jnp
from jax import lax
from jax.experimental import pallas as pl
from jax.experimental.pallas import tpu as pltpu

F32 = jnp.float32
BF16 = jnp.bfloat16

LANES = 128
SUBLANES = 8
VMEM_LIMIT_BYTES = 56 * 1024 * 1024

D_MODEL = 1024
D_CONV = 1024
SHORT_CONV = 3
D_SSM = 1024
SSM_HEADDIM = 64
SSM_HEADS = D_SSM // SSM_HEADDIM
SSM_GROUPS = 2
SSM_STATE = 128
SSM_CONV = 4
SSM_CHUNK = 128
D_XBC = D_SSM + 2 * SSM_GROUPS * SSM_STATE
D_MAIN = 3 * D_CONV + D_SSM + D_XBC
D_IN_PAD = D_MAIN + LANES
PEER_HEADS = 8
PEER_KEYS = 128
PEER_TOPK = 16
TOPK_SHIFT = PEER_TOPK.bit_length() - 1
PEER_DHALF = 128
PEER_PICKS = PEER_HEADS * PEER_TOPK
PLE_DIM = 256
EPS = 1e-6
CONV_TAIL = SUBLANES


def _rms(x, w):
    return x * lax.rsqrt(jnp.mean(x * x, axis=-1, keepdims=True) + EPS) * w


def _split3(x):
    x1 = x.astype(BF16)
    r1 = x - x1.astype(F32)
    x2 = r1.astype(BF16)
    x3 = (r1 - x2.astype(F32)).astype(BF16)
    return x1, x2, x3


def _dot(a, b):
    return jnp.dot(a, b, preferred_element_type=F32)


def _dot_nt(a, b):
    return lax.dot_general(a, b, (((1,), (1,)), ((), ())), preferred_element_type=F32)


def _sel_right(x, m):
    x1, x2, x3 = _split3(x)
    return _dot(x1, m) + _dot(x2, m) + _dot(x3, m)


def _sel_left(m, x):
    x1, x2, x3 = _split3(x)
    return _dot(m, x1) + _dot(m, x2) + _dot(m, x3)


def _sel_right_nt(x, m):
    x1, x2, x3 = _split3(x)
    return _dot_nt(x1, m) + _dot_nt(x2, m) + _dot_nt(x3, m)


def _mixer_body(h_ref, nw_ref, win_ref, cw_ref, mw_ref, mb_ref, dtb_ref, alog_ref, dexp_ref, cnw_ref, snw_ref,
                wout_ref, tril_ref, e64_ref, e128_ref, o_ref,
                proj_scr, gch_scr, xraw_scr, xbc_scr, y_scr, st_scr, *, ts):
    j = pl.program_id(1)

    @pl.when(j == 0)
    def _():
        gch_scr[0:CONV_TAIL, :] = jnp.zeros((CONV_TAIL, D_CONV), F32)
        xraw_scr[0:CONV_TAIL, :] = jnp.zeros((CONV_TAIL, D_XBC), F32)
        st_scr[...] = jnp.zeros(st_scr.shape, F32)

    x = h_ref[...]
    xn = _rms(x, nw_ref[...])
    proj_scr[...] = _dot(xn.astype(BF16), win_ref[...])

    gch_scr[CONV_TAIL:CONV_TAIL + ts, :] = proj_scr[:, D_CONV:2 * D_CONV] * proj_scr[:, 2 * D_CONV:3 * D_CONV]
    conv = cw_ref[0:1, :] * gch_scr[CONV_TAIL - 2:CONV_TAIL - 2 + ts, :]
    for jj in range(1, SHORT_CONV):
        off = CONV_TAIL - (SHORT_CONV - 1) + jj
        conv = conv + cw_ref[jj:jj + 1, :] * gch_scr[off:off + ts, :]
    y_conv = _rms(proj_scr[:, 0:D_CONV] * conv, cnw_ref[...])
    gch_scr[0:CONV_TAIL, :] = gch_scr[ts:ts + CONV_TAIL, :]

    xraw_scr[CONV_TAIL:CONV_TAIL + ts, :] = proj_scr[:, 4 * D_CONV:4 * D_CONV + D_XBC]
    xc = mb_ref[...] + mw_ref[0:1, :] * xraw_scr[CONV_TAIL - 3:CONV_TAIL - 3 + ts, :]
    for jj in range(1, SSM_CONV):
        off = CONV_TAIL - (SSM_CONV - 1) + jj
        xc = xc + mw_ref[jj:jj + 1, :] * xraw_scr[off:off + ts, :]
    xbc_scr[...] = xc * jax.nn.sigmoid(xc)
    xraw_scr[0:CONV_TAIL, :] = xraw_scr[ts:ts + CONV_TAIL, :]

    tril = tril_ref[...]
    causal = tril > 0
    a_neg = -jnp.exp(alog_ref[...])
    half = SSM_HEADS // SSM_GROUPS * SSM_HEADDIM
    lane = lax.broadcasted_iota(jnp.int32, (SSM_CHUNK, LANES), 1)

    for c in range(ts // SSM_CHUNK):
        r0 = c * SSM_CHUNK
        rows = slice(r0, r0 + SSM_CHUNK)
        xs = xbc_scr[rows, 0:D_SSM]
        dt_in = proj_scr[rows, D_MAIN:D_MAIN + LANES] + dtb_ref[...]
        dt = jnp.maximum(dt_in, 0.0) + jnp.log1p(jnp.exp(-jnp.abs(dt_in)))
        a = dt * a_neg
        a_cum = _sel_left(tril, a)
        a_cum_t = _sel_right_nt(a.T, tril)
        tot = a_cum[SSM_CHUNK - 1:SSM_CHUNK, :]
        dt_e = _sel_right(dt, e64_ref[...])
        dec_in_e = _sel_right(jnp.exp(tot - a_cum), e64_ref[...])
        dec_out_e = _sel_right(jnp.exp(a_cum), e64_ref[...])
        cd_e = _sel_right(jnp.broadcast_to(jnp.exp(tot), (SUBLANES, LANES)), e64_ref[...])[0:1, :]
        col_b = _sel_right(a_cum, e128_ref[...])
        xdt = xs * dt_e
        xdt_b = xdt.astype(BF16)
        xdtd_b = (xdt * dec_in_e).astype(BF16)
        for g in range(SSM_GROUPS):
            bg = xbc_scr[rows, D_SSM + g * SSM_STATE:D_SSM + (g + 1) * SSM_STATE]
            cg = xbc_scr[rows, D_SSM + (SSM_GROUPS + g) * SSM_STATE:D_SSM + (SSM_GROUPS + g + 1) * SSM_STATE]
            bg_b = bg.astype(BF16)
            cg_b = cg.astype(BF16)
            cb = _dot_nt(cg_b, bg_b)
            gcols = slice(g * half, (g + 1) * half)
            s_in = st_scr[g]
            y_off = _dot(cg_b, s_in.astype(BF16)) * dec_out_e[:, gcols]
            st_scr[g] = s_in * cd_e[:, gcols] + _dot(bg.T.astype(BF16), xdtd_b[:, gcols])
            for hp in range(SSM_HEADS // SSM_GROUPS // 2):
                h0 = g * (SSM_HEADS // SSM_GROUPS) + 2 * hp
                pcols = slice(h0 * SSM_HEADDIM, h0 * SSM_HEADDIM + LANES)
                res = []
                for hh in (h0, h0 + 1):
                    seg = col_b[:, hh * LANES:(hh + 1) * LANES] - a_cum_t[hh:hh + 1, :]
                    decay = jnp.exp(jnp.where(causal, seg, -jnp.inf))
                    res.append(_dot((cb * decay).astype(BF16), xdt_b[:, pcols]))
                y_diag = jnp.where(lane < SSM_HEADDIM, res[0], res[1])
                ocols = slice(hp * LANES, (hp + 1) * LANES)
                y_scr[rows, pcols] = y_diag + y_off[:, ocols] + xs[:, pcols] * dexp_ref[:, pcols]

    z = proj_scr[:, 3 * D_CONV:3 * D_CONV + D_SSM]
    y_ssd = _rms(y_scr[...] * (z * jax.nn.sigmoid(z)), snw_ref[...])
    out = _dot(y_conv.astype(BF16), wout_ref[0:D_CONV, :]) + _dot(y_ssd.astype(BF16), wout_ref[D_CONV:D_CONV + D_SSM, :])
    o_ref[...] = x + out


def _mixer(h, batch, seq, nw, win_b, cw, mw, mb, dtb, alog, dexp, cnw, snw, wout_b, tril, e64, e128, *, ts):
    n = batch * seq
    nts = seq // ts
    const = lambda shape: pl.BlockSpec(shape, lambda b, j: (0,) * len(shape))
    row = pl.BlockSpec((ts, D_MODEL), lambda b, j: (b * nts + j, 0))
    return pl.pallas_call(
        functools.partial(_mixer_body, ts=ts),
        grid=(batch, nts),
        in_specs=[row, const((1, D_MODEL)), const((D_MODEL, D_IN_PAD)), const((SUBLANES, D_CONV)),
                  const((SUBLANES, D_XBC)), const((1, D_XBC)), const((1, LANES)), const((1, LANES)),
                  const((1, D_SSM)), const((1, D_CONV)), const((1, D_SSM)), const((D_CONV + D_SSM, D_MODEL)),
                  const((SSM_CHUNK, SSM_CHUNK)), const((LANES, D_SSM)), const((LANES, SSM_HEADS * LANES))],
        out_specs=row,
        out_shape=jax.ShapeDtypeStruct((n, D_MODEL), F32),
        scratch_shapes=[pltpu.VMEM((ts, D_IN_PAD), F32), pltpu.VMEM((ts + CONV_TAIL, D_CONV), F32),
                        pltpu.VMEM((ts + CONV_TAIL, D_XBC), F32), pltpu.VMEM((ts, D_XBC), F32),
                        pltpu.VMEM((ts, D_SSM), F32),
                        pltpu.VMEM((SSM_GROUPS, SSM_STATE, D_SSM // SSM_GROUPS), F32)],
        compiler_params=pltpu.CompilerParams(dimension_semantics=("arbitrary", "arbitrary"),
                                             vmem_limit_bytes=VMEM_LIMIT_BYTES),
        name="mixer",
    )(h, nw, win_b, cw, mw, mb, dtb, alog, dexp, cnw, snw, wout_b, tril, e64, e128)


def _topk_rows(s, k, ids):
    big = jnp.iinfo(jnp.int32).max
    vals, idxs = [], []
    for _ in range(k):
        m = jnp.max(s, axis=0, keepdims=True)
        idx = jnp.min(jnp.where(s == m, ids, big), axis=0, keepdims=True)
        vals.append(m)
        idxs.append(idx)
        s = jnp.where(ids == idx, -jnp.inf, s)
    return jnp.concatenate(vals, axis=0), jnp.concatenate(idxs, axis=0)


CAND_LEN = [PEER_TOPK // (a + 1) for a in range(PEER_TOPK)]
CAND_ROWS = sum(CAND_LEN)
CAND_ROWS_PAD = -(-CAND_ROWS // SUBLANES) * SUBLANES


def _take_rows(table, sel):
    out = jnp.zeros(sel.shape, table.dtype)
    for jrow in range(table.shape[0]):
        out = jnp.where(sel == jrow, table[jrow:jrow + 1, :], out)
    return out


def _route_body(h_ref, nw_ref, wq_ref, keys_ref, xn_ref, eidx_ref, gate_ref, q_scr, eidx_scr, gate_scr):
    tr = h_ref.shape[0]
    xn = _rms(h_ref[...], nw_ref[...])
    xn_ref[...] = xn
    q_scr[...] = _dot(xn.astype(BF16), wq_ref[...]).astype(BF16)
    key_ids = lax.broadcasted_iota(jnp.int32, (PEER_KEYS, tr), 0)
    crow = lax.broadcasted_iota(jnp.int32, (CAND_ROWS_PAD, tr), 0)
    cand_ids = crow
    start = 0
    for a in range(1, PEER_TOPK):
        start += CAND_LEN[a - 1]
        cand_ids = cand_ids + jnp.where(crow >= start, PEER_TOPK - CAND_LEN[a - 1], 0)
    pad_rows = jnp.full((CAND_ROWS_PAD - CAND_ROWS, tr), -jnp.inf, F32)
    for hd in range(PEER_HEADS):
        sv, si = [], []
        for half in range(2):
            hc = hd * 2 + half
            sc_t = _dot_nt(keys_ref[hc], q_scr[:, hc * PEER_DHALF:(hc + 1) * PEER_DHALF])
            v, i = _topk_rows(sc_t, PEER_TOPK, key_ids)
            sv.append(v)
            si.append(i)
        cand = jnp.concatenate([sv[0][a:a + 1, :] + sv[1][0:CAND_LEN[a], :] for a in range(PEER_TOPK)] + [pad_rows],
                               axis=0)
        tv, ti = _topk_rows(cand, PEER_TOPK, cand_ids)
        i1 = _take_rows(si[0], lax.shift_right_logical(ti, TOPK_SHIFT))
        i2 = _take_rows(si[1], lax.bitwise_and(ti, PEER_TOPK - 1))
        ex = jnp.exp(tv - tv[0:1, :])
        rows = slice(hd * PEER_TOPK, (hd + 1) * PEER_TOPK)
        eidx_scr[rows, :] = i1 * PEER_KEYS + i2
        gate_scr[rows, :] = ex / jnp.sum(ex, axis=0, keepdims=True)
    eidx_ref[...] = eidx_scr[...].T
    gate_ref[...] = gate_scr[...].T


def _route(h, nw, wq_b, keys_b, *, tr):
    n = h.shape[0]
    return pl.pallas_call(
        _route_body,
        grid=(n // tr,),
        in_specs=[pl.BlockSpec((tr, D_MODEL), lambda i: (i, 0)),
                  pl.BlockSpec((1, D_MODEL), lambda i: (0, 0)),
                  pl.BlockSpec((D_MODEL, 2 * PEER_HEADS * PEER_DHALF), lambda i: (0, 0)),
                  pl.BlockSpec((2 * PEER_HEADS, PEER_KEYS, PEER_DHALF), lambda i: (0, 0, 0))],
        out_specs=[pl.BlockSpec((tr, D_MODEL), lambda i: (i, 0)),
                   pl.BlockSpec((tr, PEER_PICKS), lambda i: (i, 0)),
                   pl.BlockSpec((tr, PEER_PICKS), lambda i: (i, 0))],
        out_shape=[jax.ShapeDtypeStruct((n, D_MODEL), F32),
                   jax.ShapeDtypeStruct((n, PEER_PICKS), jnp.int32),
                   jax.ShapeDtypeStruct((n, PEER_PICKS), F32)],
        scratch_shapes=[pltpu.VMEM((tr, 2 * PEER_HEADS * PEER_DHALF), BF16),
                        pltpu.VMEM((PEER_PICKS, tr), jnp.int32), pltpu.VMEM((PEER_PICKS, tr), F32)],
        compiler_params=pltpu.CompilerParams(dimension_semantics=("arbitrary",),
                                             vmem_limit_bytes=VMEM_LIMIT_BYTES),
        name="route",
    )(h, nw, wq_b, keys_b)


ROW_TILES = D_MODEL // LANES


def _expert_body(idx_hbm, gate_hbm, xn_ref, h_ref, tab_hbm, o_ref,
                 idx0, idx1, gate0, gate1, buf0, buf1, part_scr, w_scr, isem, gsem, rsem, *, g_tok, nsteps):
    i = pl.program_id(0)
    npick = PEER_PICKS
    idx_s, gate_s, bufs = (idx0, idx1), (gate0, gate1), (buf0, buf1)

    def idx_copy(grp, s):
        return pltpu.make_async_copy(idx_hbm.at[grp], idx_s[s], isem.at[s])

    def gate_copy(grp, s):
        return pltpu.make_async_copy(gate_hbm.at[grp], gate_s[s], gsem.at[s])

    def rows_done(s):
        return pltpu.make_async_copy(tab_hbm.at[pl.ds(0, g_tok * npick)], bufs[s], rsem.at[s])

    def start_row(s, t, k):
        pltpu.make_async_copy(tab_hbm.at[idx_s[s][t, k]], bufs[s].at[t * npick + k], rsem.at[s]).start()

    def token_loop(issue_slot, comp_slot):
        per_half = npick // 2

        def body(t, carry):
            if comp_slot is None:
                for k in range(npick):
                    start_row(issue_slot, t, k)
                return carry
            buf, gate = bufs[comp_slot], gate_s[comp_slot]
            xt = xn_ref[t]
            base = t * npick
            for k in range(npick):
                if issue_slot is not None and k % 2 == 0:
                    start_row(issue_slot, t, k // 2)
                part_scr[k:k + 1, :] = jnp.sum(buf[base + k, 0:ROW_TILES, :] * xt, axis=0, keepdims=True)
            act = jnp.sum(part_scr[...], axis=-1, keepdims=True)
            w_scr[...] = jnp.broadcast_to(jax.nn.gelu(act), (npick, LANES))
            accs = [jnp.zeros((ROW_TILES, LANES), F32) for _ in range(4)]
            for k in range(npick):
                if issue_slot is not None and k % 2 == 0:
                    start_row(issue_slot, t, per_half + k // 2)
                coef = w_scr[k:k + 1, :] * gate[t, k]
                accs[k % 4] = accs[k % 4] + coef * buf[base + k, ROW_TILES:2 * ROW_TILES, :]
            o_ref[t] = h_ref[t] + ((accs[0] + accs[1]) + (accs[2] + accs[3]))
            return carry

        lax.fori_loop(0, g_tok, body, 0)

    def step(s, first, last):
        if not last:
            if first:
                idx_copy(i, s).start()
            idx_copy(i, s).wait()

            @pl.when(i + 1 < nsteps)
            def _():
                idx_copy(i + 1, 1 - s).start()

            gate_copy(i, s).start()
        if not first:
            gate_copy(i - 1, 1 - s).wait()
            rows_done(1 - s).wait()
        token_loop(None if last else s, None if first else 1 - s)

    parity = lax.rem(i, 2)
    pl.when(i == 0)(lambda: step(0, True, False))
    pl.when((i > 0) & (i < nsteps) & (parity == 0))(lambda: step(0, False, False))
    pl.when((i < nsteps) & (parity == 1))(lambda: step(1, False, False))
    pl.when(i == nsteps)(lambda: step(nsteps % 2, False, True))


def _experts(eidx, gate, xn, h, table, *, g_tok):
    n = xn.shape[0]
    nsteps = n // g_tok
    idx3 = eidx.reshape(nsteps, g_tok, PEER_PICKS)
    gate3 = gate.reshape(nsteps, g_tok, PEER_PICKS)
    xn3 = xn.reshape(n, ROW_TILES, LANES)
    h3 = h.reshape(n, ROW_TILES, LANES)
    tok = pl.BlockSpec((g_tok, ROW_TILES, LANES), lambda i: (jnp.maximum(i - 1, 0), 0, 0))
    slot_rows = pltpu.VMEM((g_tok * PEER_PICKS, 2 * ROW_TILES, LANES), F32)
    out = pl.pallas_call(
        functools.partial(_expert_body, g_tok=g_tok, nsteps=nsteps),
        grid=(nsteps + 1,),
        in_specs=[pl.BlockSpec(memory_space=pl.ANY), pl.BlockSpec(memory_space=pl.ANY), tok, tok,
                  pl.BlockSpec(memory_space=pl.ANY)],
        out_specs=tok,
        out_shape=jax.ShapeDtypeStruct((n, ROW_TILES, LANES), F32),
        scratch_shapes=[pltpu.SMEM((g_tok, PEER_PICKS), jnp.int32), pltpu.SMEM((g_tok, PEER_PICKS), jnp.int32),
                        pltpu.SMEM((g_tok, PEER_PICKS), F32), pltpu.SMEM((g_tok, PEER_PICKS), F32),
                        slot_rows, slot_rows,
                        pltpu.VMEM((PEER_PICKS, LANES), F32),
                        pltpu.VMEM((PEER_PICKS, LANES), F32),
                        pltpu.SemaphoreType.DMA((2,)), pltpu.SemaphoreType.DMA((2,)),
                        pltpu.SemaphoreType.DMA((2,))],
        compiler_params=pltpu.CompilerParams(dimension_semantics=("arbitrary",),
                                             vmem_limit_bytes=VMEM_LIMIT_BYTES),
        name="experts",
    )(idx3, gate3, xn3, h3, table)
    return out.reshape(n, D_MODEL)


def _ple_body(h_ref, p_ref, nw_ref, wg_ref, wp_ref, fw_ref, o_ref, *, final):
    x = h_ref[...]
    gate = jax.nn.sigmoid(_dot(_rms(x, nw_ref[...]).astype(BF16), wg_ref[...]))
    out = x + gate * _dot(p_ref[...].astype(BF16), wp_ref[...])
    o_ref[...] = _rms(out, fw_ref[...]) if final else out


def _ple(h, p, nw, wg_b, wp_b, fw, *, final, tp):
    n = h.shape[0]
    row = lambda d: pl.BlockSpec((tp, d), lambda i: (i, 0))
    const = lambda shape: pl.BlockSpec(shape, lambda i: (0, 0))
    return pl.pallas_call(
        functools.partial(_ple_body, final=final),
        grid=(n // tp,),
        in_specs=[row(D_MODEL), row(PLE_DIM), const((1, D_MODEL)), const((D_MODEL, D_MODEL)),
                  const((PLE_DIM, D_MODEL)), const((1, D_MODEL))],
        out_specs=row(D_MODEL),
        out_shape=jax.ShapeDtypeStruct((n, D_MODEL), F32),
        compiler_params=pltpu.CompilerParams(dimension_semantics=("arbitrary",),
                                             vmem_limit_bytes=VMEM_LIMIT_BYTES),
        name="ple",
    )(h, p, nw, wg_b, wp_b, fw)


def _tile_sizes(seq):
    ts = 256 if seq % 256 == 0 else SSM_CHUNK
    return dict(ts=ts, tr=256 if seq % 256 == 0 else LANES, g_tok=SUBLANES, tp=512 if seq % 512 == 0 else LANES)


def _pad_rows(w, rows):
    return jnp.pad(w, ((0, rows - w.shape[0]), (0, 0)))


def _pad_lanes(v):
    return jnp.pad(v, (0, LANES - v.shape[0])).reshape(1, LANES)


def kernel(x, p, norm_mix_w, w_in, conv_w, mconv_w, mconv_b, dt_bias, a_log, d_skip, conv_norm_w, ssd_norm_w,
           w_out, norm_ffn_w, peer_wq, peer_keys, peer_u, peer_v, norm_ple_w, w_ple_gate, w_ple, final_norm_w):
    batch, seq, _ = x.shape
    depth = w_in.shape[0]
    n = batch * seq
    tiles = _tile_sizes(seq)
    n_exp = peer_u.shape[1]

    li = lax.broadcasted_iota(jnp.int32, (SSM_CHUNK, SSM_CHUNK), 0)
    si = lax.broadcasted_iota(jnp.int32, (SSM_CHUNK, SSM_CHUNK), 1)
    tril = (li >= si).astype(BF16)
    hrow = lax.broadcasted_iota(jnp.int32, (LANES, D_SSM), 0)
    hcol = lax.broadcasted_iota(jnp.int32, (LANES, D_SSM), 1)
    e64 = (hcol // SSM_HEADDIM == hrow).astype(BF16)
    hrow2 = lax.broadcasted_iota(jnp.int32, (LANES, SSM_HEADS * LANES), 0)
    hcol2 = lax.broadcasted_iota(jnp.int32, (LANES, SSM_HEADS * LANES), 1)
    e128 = (hcol2 // LANES == hrow2).astype(BF16)

    h = x.reshape(n, D_MODEL)
    for i in range(depth):
        win_b = jnp.pad(w_in[i], ((0, 0), (0, D_IN_PAD - w_in.shape[2]))).astype(BF16)
        h = _mixer(h, batch, seq, norm_mix_w[i].reshape(1, -1), win_b,
                   _pad_rows(conv_w[i], SUBLANES), _pad_rows(mconv_w[i], SUBLANES), mconv_b[i].reshape(1, -1),
                   _pad_lanes(dt_bias[i]), _pad_lanes(a_log[i]), jnp.repeat(d_skip[i], SSM_HEADDIM).reshape(1, -1),
                   conv_norm_w[i].reshape(1, -1), ssd_norm_w[i].reshape(1, -1), w_out[i].astype(BF16),
                   tril, e64, e128, ts=tiles["ts"])
        keys_b = peer_keys[i].reshape(2 * PEER_HEADS, PEER_KEYS, PEER_DHALF).astype(BF16)
        xn, eidx, gate = _route(h, norm_ffn_w[i].reshape(1, -1), peer_wq[i].astype(BF16), keys_b, tr=tiles["tr"])
        table = jnp.concatenate([peer_u[i].reshape(n_exp, ROW_TILES, LANES),
                                 peer_v[i].reshape(n_exp, ROW_TILES, LANES)], axis=1)
        h = _experts(eidx, gate, xn, h, table, g_tok=tiles["g_tok"])
        h = _ple(h, p[i].reshape(n, PLE_DIM), norm_ple_w[i].reshape(1, -1), w_ple_gate[i].astype(BF16),
                 w_ple[i].astype(BF16), final_norm_w.reshape(1, -1), final=(i == depth - 1), tp=tiles["tp"])
    return h.reshape(batch, seq, D_MODEL)
```

```python
import functools

import jax
import jax.numpy as jnp
from jax import lax
from jax.experimental import pallas as pl
from jax.experimental.pallas import tpu as pltpu

F32 = jnp.float32
BF16 = jnp.bfloat16

LANES = 128
SUBLANES = 8
VMEM_LIMIT_BYTES = 56 * 1024 * 1024

D_MODEL = 1024
D_CONV = 1024
SHORT_CONV = 3
D_SSM = 1024
SSM_HEADDIM = 64
SSM_HEADS = D_SSM // SSM_HEADDIM
SSM_GROUPS = 2
SSM_STATE = 128
SSM_CONV = 4
SSM_CHUNK = 128
D_XBC = D_SSM + 2 * SSM_GROUPS * SSM_STATE
D_MAIN = 3 * D_CONV + D_SSM + D_XBC
D_IN_PAD = D_MAIN + LANES
PEER_HEADS = 8
PEER_KEYS = 128
PEER_TOPK = 16
TOPK_SHIFT = PEER_TOPK.bit_length() - 1
PEER_DHALF = 128
PEER_PICKS = PEER_HEADS * PEER_TOPK
PLE_DIM = 256
EPS = 1e-6
CONV_TAIL = SUBLANES


def _rms(x, w):
    return x * lax.rsqrt(jnp.mean(x * x, axis=-1, keepdims=True) + EPS) * w


def _split3(x):
    x1 = x.astype(BF16)
    r1 = x - x1.astype(F32)
    x2 = r1.astype(BF16)
    x3 = (r1 - x2.astype(F32)).astype(BF16)
    return x1, x2, x3


def _dot(a, b):
    return jnp.dot(a, b, preferred_element_type=F32)


def _dot_nt(a, b):
    return lax.dot_general(a, b, (((1,), (1,)), ((), ())), preferred_element_type=F32)


def _sel_right(x, m):
    x1, x2, x3 = _split3(x)
    return _dot(x1, m) + _dot(x2, m) + _dot(x3, m)


def _sel_left(m, x):
    x1, x2, x3 = _split3(x)
    return _dot(m, x1) + _dot(m, x2) + _dot(m, x3)


def _sel_right_nt(x, m):
    x1, x2, x3 = _split3(x)
    return _dot_nt(x1, m) + _dot_nt(x2, m) + _dot_nt(x3, m)


def _mixer_body(h_ref, nw_ref, win_ref, cw_ref, mw_ref, mb_ref, dtb_ref, alog_ref, dexp_ref, cnw_ref, snw_ref,
                wout_ref, tril_ref, e64_ref, e128_ref, o_ref,
                proj_scr, gch_scr, xraw_scr, xbc_scr, y_scr, st_scr, *, ts):
    j = pl.program_id(1)

    @pl.when(j == 0)
    def _():
        gch_scr[0:CONV_TAIL, :] = jnp.zeros((CONV_TAIL, D_CONV), F32)
        xraw_scr[0:CONV_TAIL, :] = jnp.zeros((CONV_TAIL, D_XBC), F32)
        st_scr[...] = jnp.zeros(st_scr.shape, F32)

    x = h_ref[...]
    xn = _rms(x, nw_ref[...])
    proj_scr[...] = _dot(xn.astype(BF16), win_ref[...])

    gch_scr[CONV_TAIL:CONV_TAIL + ts, :] = proj_scr[:, D_CONV:2 * D_CONV] * proj_scr[:, 2 * D_CONV:3 * D_CONV]
    conv = cw_ref[0:1, :] * gch_scr[CONV_TAIL - 2:CONV_TAIL - 2 + ts, :]
    for jj in range(1, SHORT_CONV):
        off = CONV_TAIL - (SHORT_CONV - 1) + jj
        conv = conv + cw_ref[jj:jj + 1, :] * gch_scr[off:off + ts, :]
    y_conv = _rms(proj_scr[:, 0:D_CONV] * conv, cnw_ref[...])
    gch_scr[0:CONV_TAIL, :] = gch_scr[ts:ts + CONV_TAIL, :]

    xraw_scr[CONV_TAIL:CONV_TAIL + ts, :] = proj_scr[:, 4 * D_CONV:4 * D_CONV + D_XBC]
    xc = mb_ref[...] + mw_ref[0:1, :] * xraw_scr[CONV_TAIL - 3:CONV_TAIL - 3 + ts, :]
    for jj in range(1, SSM_CONV):
        off = CONV_TAIL - (SSM_CONV - 1) + jj
        xc = xc + mw_ref[jj:jj + 1, :] * xraw_scr[off:off + ts, :]
    xbc_scr[...] = xc * jax.nn.sigmoid(xc)
    xraw_scr[0:CONV_TAIL, :] = xraw_scr[ts:ts + CONV_TAIL, :]

    tril = tril_ref[...]
    causal = tril > 0
    a_neg = -jnp.exp(alog_ref[...])
    half = SSM_HEADS // SSM_GROUPS * SSM_HEADDIM
    lane = lax.broadcasted_iota(jnp.int32, (SSM_CHUNK, LANES), 1)

    for c in range(ts // SSM_CHUNK):
        r0 = c * SSM_CHUNK
        rows = slice(r0, r0 + SSM_CHUNK)
        xs = xbc_scr[rows, 0:D_SSM]
        dt_in = proj_scr[rows, D_MAIN:D_MAIN + LANES] + dtb_ref[...]
        dt = jnp.maximum(dt_in, 0.0) + jnp.log1p(jnp.exp(-jnp.abs(dt_in)))
        a = dt * a_neg
        a_cum = _sel_left(tril, a)
        a_cum_t = _sel_right_nt(a.T, tril)
        tot = a_cum[SSM_CHUNK - 1:SSM_CHUNK, :]
        dt_e = _sel_right(dt, e64_ref[...])
        dec_in_e = _sel_right(jnp.exp(tot - a_cum), e64_ref[...])
        dec_out_e = _sel_right(jnp.exp(a_cum), e64_ref[...])
        cd_e = _sel_right(jnp.broadcast_to(jnp.exp(tot), (SUBLANES, LANES)), e64_ref[...])[0:1, :]
        col_b = _sel_right(a_cum, e128_ref[...])
        xdt = xs * dt_e
        xdt_b = xdt.astype(BF16)
        xdtd_b = (xdt * dec_in_e).astype(BF16)
        for g in range(SSM_GROUPS):
            bg = xbc_scr[rows, D_SSM + g * SSM_STATE:D_SSM + (g + 1) * SSM_STATE]
            cg = xbc_scr[rows, D_SSM + (SSM_GROUPS + g) * SSM_STATE:D_SSM + (SSM_GROUPS + g + 1) * SSM_STATE]
            bg_b = bg.astype(BF16)
            cg_b = cg.astype(BF16)
            cb = _dot_nt(cg_b, bg_b)
            gcols = slice(g * half, (g + 1) * half)
            s_in = st_scr[g]
            y_off = _dot(cg_b, s_in.astype(BF16)) * dec_out_e[:, gcols]
            st_scr[g] = s_in * cd_e[:, gcols] + _dot(bg.T.astype(BF16), xdtd_b[:, gcols])
            for hp in range(SSM_HEADS // SSM_GROUPS // 2):
                h0 = g * (SSM_HEADS // SSM_GROUPS) + 2 * hp
                pcols = slice(h0 * SSM_HEADDIM, h0 * SSM_HEADDIM + LANES)
                res = []
                for hh in (h0, h0 + 1):
                    seg = col_b[:, hh * LANES:(hh + 1) * LANES] - a_cum_t[hh:hh + 1, :]
                    decay = jnp.exp(jnp.where(causal, seg, -jnp.inf))
                    res.append(_dot((cb * decay).astype(BF16), xdt_b[:, pcols]))
                y_diag = jnp.where(lane < SSM_HEADDIM, res[0], res[1])
                ocols = slice(hp * LANES, (hp + 1) * LANES)
                y_scr[rows, pcols] = y_diag + y_off[:, ocols] + xs[:, pcols] * dexp_ref[:, pcols]

    z = proj_scr[:, 3 * D_CONV:3 * D_CONV + D_SSM]
    y_ssd = _rms(y_scr[...] * (z * jax.nn.sigmoid(z)), snw_ref[...])
    out = _dot(y_conv.astype(BF16), wout_ref[0:D_CONV, :]) + _dot(y_ssd.astype(BF16), wout_ref[D_CONV:D_CONV + D_SSM, :])
    o_ref[...] = x + out


def _mixer(h, batch, seq, nw, win_b, cw, mw, mb, dtb, alog, dexp, cnw, snw, wout_b, tril, e64, e128, *, ts):
    n = batch * seq
    nts = seq // ts
    const = lambda shape: pl.BlockSpec(shape, lambda b, j: (0,) * len(shape))
    row = pl.BlockSpec((ts, D_MODEL), lambda b, j: (b * nts + j, 0))
    return pl.pallas_call(
        functools.partial(_mixer_body, ts=ts),
        grid=(batch, nts),
        in_specs=[row, const((1, D_MODEL)), const((D_MODEL, D_IN_PAD)), const((SUBLANES, D_CONV)),
                  const((SUBLANES, D_XBC)), const((1, D_XBC)), const((1, LANES)), const((1, LANES)),
                  const((1, D_SSM)), const((1, D_CONV)), const((1, D_SSM)), const((D_CONV + D_SSM, D_MODEL)),
                  const((SSM_CHUNK, SSM_CHUNK)), const((LANES, D_SSM)), const((LANES, SSM_HEADS * LANES))],
        out_specs=row,
        out_shape=jax.ShapeDtypeStruct((n, D_MODEL), F32),
        scratch_shapes=[pltpu.VMEM((ts, D_IN_PAD), F32), pltpu.VMEM((ts + CONV_TAIL, D_CONV), F32),
                        pltpu.VMEM((ts + CONV_TAIL, D_XBC), F32), pltpu.VMEM((ts, D_XBC), F32),
                        pltpu.VMEM((ts, D_SSM), F32),
                        pltpu.VMEM((SSM_GROUPS, SSM_STATE, D_SSM // SSM_GROUPS), F32)],
        compiler_params=pltpu.CompilerParams(dimension_semantics=("arbitrary", "arbitrary"),
                                             vmem_limit_bytes=VMEM_LIMIT_BYTES),
        name="mixer",
    )(h, nw, win_b, cw, mw, mb, dtb, alog, dexp, cnw, snw, wout_b, tril, e64, e128)


def _topk_rows(s, k, ids):
    big = jnp.float32(2 ** 24)
    vals, idxs = [], []
    for _ in range(k):
        m = jnp.max(s, axis=0, keepdims=True)
        idx = jnp.min(jnp.where(s == m, ids, big), axis=0, keepdims=True)
        vals.append(m)
        idxs.append(idx)
        s = jnp.where(ids == idx, -jnp.inf, s)
    return jnp.concatenate(vals, axis=0), jnp.concatenate(idxs, axis=0).astype(jnp.int32)


CAND_LEN = [PEER_TOPK // (a + 1) for a in range(PEER_TOPK)]
CAND_ROWS = sum(CAND_LEN)
CAND_ROWS_PAD = -(-CAND_ROWS // SUBLANES) * SUBLANES


def _take_rows(table, sel):
    out = jnp.zeros(sel.shape, table.dtype)
    for jrow in range(table.shape[0]):
        out = jnp.where(sel == jrow, table[jrow:jrow + 1, :], out)
    return out


def _route_body(h_ref, nw_ref, wq_ref, keys_ref, xn_ref, eidx_ref, gate_ref, q_scr, eidx_scr, gate_scr):
    tr = h_ref.shape[0]
    xn = _rms(h_ref[...], nw_ref[...])
    xn_ref[...] = xn
    q_scr[...] = _dot(xn.astype(BF16), wq_ref[...]).astype(BF16)
    key_ids = lax.broadcasted_iota(jnp.int32, (PEER_KEYS, tr), 0).astype(F32)
    crow = lax.broadcasted_iota(jnp.int32, (CAND_ROWS_PAD, tr), 0)
    cand_ids = crow
    start = 0
    for a in range(1, PEER_TOPK):
        start += CAND_LEN[a - 1]
        cand_ids = cand_ids + jnp.where(crow >= start, PEER_TOPK - CAND_LEN[a - 1], 0)
    cand_ids_f = cand_ids.astype(F32)
    pad_rows = jnp.full((CAND_ROWS_PAD - CAND_ROWS, tr), -jnp.inf, F32)
    for hd in range(PEER_HEADS):
        sv, si = [], []
        for half in range(2):
            hc = hd * 2 + half
            sc_t = _dot_nt(keys_ref[hc], q_scr[:, hc * PEER_DHALF:(hc + 1) * PEER_DHALF])
            v, i = _topk_rows(sc_t, PEER_TOPK, key_ids)
            sv.append(v)
            si.append(i)
        cand = jnp.concatenate([sv[0][a:a + 1, :] + sv[1][0:CAND_LEN[a], :] for a in range(PEER_TOPK)] + [pad_rows],
                               axis=0)
        tv, ti = _topk_rows(cand, PEER_TOPK, cand_ids_f)
        i1 = _take_rows(si[0], lax.shift_right_logical(ti, TOPK_SHIFT))
        i2 = _take_rows(si[1], lax.bitwise_and(ti, PEER_TOPK - 1))
        ex = jnp.exp(tv - tv[0:1, :])
        rows = slice(hd * PEER_TOPK, (hd + 1) * PEER_TOPK)
        eidx_scr[rows, :] = i1 * PEER_KEYS + i2
        gate_scr[rows, :] = ex / jnp.sum(ex, axis=0, keepdims=True)
    eidx_ref[...] = eidx_scr[...].T
    gate_ref[...] = gate_scr[...].T


def _route(h, nw, wq_b, keys_b, *, tr):
    n = h.shape[0]
    return pl.pallas_call(
        _route_body,
        grid=(n // tr,),
        in_specs=[pl.BlockSpec((tr, D_MODEL), lambda i: (i, 0)),
                  pl.BlockSpec((1, D_MODEL), lambda i: (0, 0)),
                  pl.BlockSpec((D_MODEL, 2 * PEER_HEADS * PEER_DHALF), lambda i: (0, 0)),
                  pl.BlockSpec((2 * PEER_HEADS, PEER_KEYS, PEER_DHALF), lambda i: (0, 0, 0))],
        out_specs=[pl.BlockSpec((tr, D_MODEL), lambda i: (i, 0)),
                   pl.BlockSpec((tr, PEER_PICKS), lambda i: (i, 0)),
                   pl.BlockSpec((tr, PEER_PICKS), lambda i: (i, 0))],
        out_shape=[jax.ShapeDtypeStruct((n, D_MODEL), F32),
                   jax.ShapeDtypeStruct((n, PEER_PICKS), jnp.int32),
                   jax.ShapeDtypeStruct((n, PEER_PICKS), F32)],
        scratch_shapes=[pltpu.VMEM((tr, 2 * PEER_HEADS * PEER_DHALF), BF16),
                        pltpu.VMEM((PEER_PICKS, tr), jnp.int32), pltpu.VMEM((PEER_PICKS, tr), F32)],
        compiler_params=pltpu.CompilerParams(dimension_semantics=("arbitrary",),
                                             vmem_limit_bytes=VMEM_LIMIT_BYTES),
        name="route",
    )(h, nw, wq_b, keys_b)


ROW_TILES = D_MODEL // LANES


EXPERT_SLOTS = 3
EXPERT_LAG = EXPERT_SLOTS - 1
BF16_BITS = 16
HIGH_HALF_MASK = -(1 << BF16_BITS)


def _pack_rows(u, v):
    ub = lax.bitcast_convert_type(u.astype(jnp.bfloat16), jnp.uint16).astype(jnp.uint32)
    vb = lax.bitcast_convert_type(v.astype(jnp.bfloat16), jnp.uint16).astype(jnp.uint32)
    words = lax.bitcast_convert_type(ub | (vb << BF16_BITS), jnp.int32)
    return words.reshape(u.shape[0], ROW_TILES, LANES)


def _sublane_sums(tiles, sub):
    assert len(tiles) == SUBLANES == 8
    m4, m2, m1 = sub < 4, (sub & 2) == 0, (sub & 1) == 0
    a = [jnp.where(m4, tiles[j], tiles[j + 4]) + pltpu.roll(jnp.where(m4, tiles[j + 4], tiles[j]), 4, axis=0)
         for j in range(4)]
    b = [jnp.where(m2, a[j] + pltpu.roll(a[j], 6, axis=0), a[j + 2] + pltpu.roll(a[j + 2], 2, axis=0))
         for j in range(2)]
    return jnp.where(m1, b[0] + pltpu.roll(b[0], 7, axis=0), b[1] + pltpu.roll(b[1], 1, axis=0))


def _expert_body(idx_hbm, gate_ref, xn_ref, h_ref, tab_hbm, o_ref, *scratch, g_tok, nsteps):
    ns = EXPERT_SLOTS
    idx_s, bufs = scratch[0:ns], scratch[ns:2 * ns]
    part_scr, w_scr, isem, rsem = scratch[2 * ns:]
    i = pl.program_id(0)
    npick = PEER_PICKS

    def idx_copy(grp, s):
        return pltpu.make_async_copy(idx_hbm.at[grp], idx_s[s], isem.at[s])

    def rows_done(s):
        return pltpu.make_async_copy(tab_hbm.at[pl.ds(0, g_tok * npick)], bufs[s], rsem.at[s])

    def start_row(s, t, k):
        pltpu.make_async_copy(tab_hbm.at[idx_s[s][t, k]], bufs[s].at[t * npick + k], rsem.at[s]).start(priority=k % 2)

    def token_loop(issue_slot, comp_slot):
        per_half = npick // 2
        eye = (lax.broadcasted_iota(jnp.int32, (npick, npick), 0)
               == lax.broadcasted_iota(jnp.int32, (npick, npick), 1)).astype(F32)
        sub = lax.broadcasted_iota(jnp.int32, (SUBLANES, LANES), 0)

        def body(t, carry):
            if comp_slot is None:
                for k in range(npick):
                    start_row(issue_slot, t, k)
                return carry
            buf = bufs[comp_slot]
            xt = xn_ref[t]
            base = t * npick
            for k0 in range(0, npick, SUBLANES):
                prods = []
                for k in range(k0, k0 + SUBLANES):
                    if issue_slot is not None and k % 2 == 0:
                        start_row(issue_slot, t, k // 2)
                    u_k = lax.bitcast_convert_type(lax.shift_left(buf[base + k], BF16_BITS), F32)
                    prods.append(u_k * xt)
                part_scr[k0:k0 + SUBLANES, :] = _sublane_sums(prods, sub)
            act = jnp.sum(part_scr[...], axis=-1, keepdims=True)
            gate_col = jnp.sum(eye * gate_ref[pl.ds(t, 1), :], axis=-1, keepdims=True)
            w_scr[...] = jnp.broadcast_to(gate_col * jax.nn.gelu(act), (npick, LANES))
            accs = [jnp.zeros((ROW_TILES, LANES), F32) for _ in range(4)]
            for k in range(npick):
                if issue_slot is not None and k % 2 == 0:
                    start_row(issue_slot, t, per_half + k // 2)
                v_k = lax.bitcast_convert_type(lax.bitwise_and(buf[base + k], HIGH_HALF_MASK), F32)
                accs[k % 4] = accs[k % 4] + w_scr[k:k + 1, :] * v_k
            o_ref[t] = h_ref[t] + ((accs[0] + accs[1]) + (accs[2] + accs[3]))
            return carry

        for t in range(g_tok):
            body(t, 0)

    def step(s, issue, comp):
        cs = (s - EXPERT_LAG) % ns
        if issue:
            @pl.when(i == 0)
            def _():
                idx_copy(i, s).start()

            idx_copy(i, s).wait()

            @pl.when(i + 1 < nsteps)
            def _():
                idx_copy(i + 1, (s + 1) % ns).start()

        if comp:
            rows_done(cs).wait()
        token_loop(s if issue else None, cs if comp else None)

    slot = lax.rem(i, ns)
    for s in range(ns):
        pl.when((slot == s) & (i < EXPERT_LAG))(functools.partial(step, s, True, False))
        pl.when((slot == s) & (i >= EXPERT_LAG) & (i < nsteps))(functools.partial(step, s, True, True))
        pl.when((slot == s) & (i >= nsteps))(functools.partial(step, s, False, True))


def _experts(eidx, gate, xn, h, table, *, g_tok):
    n = xn.shape[0]
    nsteps = n // g_tok
    idx3 = eidx.reshape(nsteps, g_tok, PEER_PICKS)
    xn3 = xn.reshape(n, ROW_TILES, LANES)
    h3 = h.reshape(n, ROW_TILES, LANES)
    lagged = lambda i: jnp.maximum(i - EXPERT_LAG, 0)
    tok = pl.BlockSpec((g_tok, ROW_TILES, LANES), lambda i: (lagged(i), 0, 0))
    ns = EXPERT_SLOTS
    out = pl.pallas_call(
        functools.partial(_expert_body, g_tok=g_tok, nsteps=nsteps),
        grid=(nsteps + EXPERT_LAG,),
        in_specs=[pl.BlockSpec(memory_space=pl.ANY), pl.BlockSpec((g_tok, PEER_PICKS), lambda i: (lagged(i), 0)),
                  tok, tok, pl.BlockSpec(memory_space=pl.ANY)],
        out_specs=tok,
        out_shape=jax.ShapeDtypeStruct((n, ROW_TILES, LANES), F32),
        scratch_shapes=([pltpu.SMEM((g_tok, PEER_PICKS), jnp.int32)] * ns
                        + [pltpu.VMEM((g_tok * PEER_PICKS, ROW_TILES, LANES), jnp.int32)] * ns
                        + [pltpu.VMEM((PEER_PICKS, LANES), F32), pltpu.VMEM((PEER_PICKS, LANES), F32),
                           pltpu.SemaphoreType.DMA((ns,)), pltpu.SemaphoreType.DMA((ns,))]),
        compiler_params=pltpu.CompilerParams(dimension_semantics=("arbitrary",),
                                             vmem_limit_bytes=VMEM_LIMIT_BYTES),
        name="experts",
    )(idx3, gate, xn3, h3, table)
    return out.reshape(n, D_MODEL)


def _ple_body(h_ref, p_ref, nw_ref, wg_ref, wp_ref, fw_ref, o_ref, *, final):
    x = h_ref[...]
    gate = jax.nn.sigmoid(_dot(_rms(x, nw_ref[...]).astype(BF16), wg_ref[...]))
    out = x + gate * _dot(p_ref[...].astype(BF16), wp_ref[...])
    o_ref[...] = _rms(out, fw_ref[...]) if final else out


def _ple(h, p, nw, wg_b, wp_b, fw, *, final, tp):
    n = h.shape[0]
    row = lambda d: pl.BlockSpec((tp, d), lambda i: (i, 0))
    const = lambda shape: pl.BlockSpec(shape, lambda i: (0, 0))
    return pl.pallas_call(
        functools.partial(_ple_body, final=final),
        grid=(n // tp,),
        in_specs=[row(D_MODEL), row(PLE_DIM), const((1, D_MODEL)), const((D_MODEL, D_MODEL)),
                  const((PLE_DIM, D_MODEL)), const((1, D_MODEL))],
        out_specs=row(D_MODEL),
        out_shape=jax.ShapeDtypeStruct((n, D_MODEL), F32),
        compiler_params=pltpu.CompilerParams(dimension_semantics=("arbitrary",),
                                             vmem_limit_bytes=VMEM_LIMIT_BYTES),
        name="ple",
    )(h, p, nw, wg_b, wp_b, fw)


def _tile_sizes(seq):
    ts = 256 if seq % 256 == 0 else SSM_CHUNK
    return dict(ts=ts, tr=256 if seq % 256 == 0 else LANES, g_tok=SUBLANES, tp=512 if seq % 512 == 0 else LANES)


def _pad_rows(w, rows):
    return jnp.pad(w, ((0, rows - w.shape[0]), (0, 0)))


def _pad_lanes(v):
    return jnp.pad(v, (0, LANES - v.shape[0])).reshape(1, LANES)


def kernel(x, p, norm_mix_w, w_in, conv_w, mconv_w, mconv_b, dt_bias, a_log, d_skip, conv_norm_w, ssd_norm_w,
           w_out, norm_ffn_w, peer_wq, peer_keys, peer_u, peer_v, norm_ple_w, w_ple_gate, w_ple, final_norm_w):
    batch, seq, _ = x.shape
    depth = w_in.shape[0]
    n = batch * seq
    tiles = _tile_sizes(seq)
    n_exp = peer_u.shape[1]

    li = lax.broadcasted_iota(jnp.int32, (SSM_CHUNK, SSM_CHUNK), 0)
    si = lax.broadcasted_iota(jnp.int32, (SSM_CHUNK, SSM_CHUNK), 1)
    tril = (li >= si).astype(BF16)
    hrow = lax.broadcasted_iota(jnp.int32, (LANES, D_SSM), 0)
    hcol = lax.broadcasted_iota(jnp.int32, (LANES, D_SSM), 1)
    e64 = (hcol // SSM_HEADDIM == hrow).astype(BF16)
    hrow2 = lax.broadcasted_iota(jnp.int32, (LANES, SSM_HEADS * LANES), 0)
    hcol2 = lax.broadcasted_iota(jnp.int32, (LANES, SSM_HEADS * LANES), 1)
    e128 = (hcol2 // LANES == hrow2).astype(BF16)

    h = x.reshape(n, D_MODEL)
    for i in range(depth):
        win_b = jnp.pad(w_in[i], ((0, 0), (0, D_IN_PAD - w_in.shape[2]))).astype(BF16)
        h = _mixer(h, batch, seq, norm_mix_w[i].reshape(1, -1), win_b,
                   _pad_rows(conv_w[i], SUBLANES), _pad_rows(mconv_w[i], SUBLANES), mconv_b[i].reshape(1, -1),
                   _pad_lanes(dt_bias[i]), _pad_lanes(a_log[i]), jnp.repeat(d_skip[i], SSM_HEADDIM).reshape(1, -1),
                   conv_norm_w[i].reshape(1, -1), ssd_norm_w[i].reshape(1, -1), w_out[i].astype(BF16),
                   tril, e64, e128, ts=tiles["ts"])
        keys_b = peer_keys[i].reshape(2 * PEER_HEADS, PEER_KEYS, PEER_DHALF).astype(BF16)
        xn, eidx, gate = _route(h, norm_ffn_w[i].reshape(1, -1), peer_wq[i].astype(BF16), keys_b, tr=tiles["tr"])
        table = _pack_rows(peer_u[i], peer_v[i])
        h = _experts(eidx, gate, xn, h, table, g_tok=tiles["g_tok"])
        h = _ple(h, p[i].reshape(n, PLE_DIM), norm_ple_w[i].reshape(1, -1), w_ple_gate[i].astype(BF16),
                 w_ple[i].astype(BF16), final_norm_w.reshape(1, -1), final=(i == depth - 1), tp=tiles["tp"])
    return h.reshape(batch, seq, D_MODEL)
```
